```python
import jax, jax.numpy as jnp
from jax import lax
import numpy as np

D_MODEL = 2048
BATCH = 4
SEQ = 2048
DEPTH = 1
DEC_BATCH = 128
DEC_SEQ = 4
PAST_LEN = 16384
PAGE_SIZE = 128

GLA_HEADS = 4
GLA_DK_TOTAL = D_MODEL // 2
GLA_DV_TOTAL = D_MODEL
GLA_DK = GLA_DK_TOTAL // GLA_HEADS
GLA_DV = GLA_DV_TOTAL // GLA_HEADS
GLA_GATE_RANK = 16
GLA_GATE_TAU = 16.0
GLA_CHUNK = 16
CONV_CHANNELS = D_MODEL
CONV_WIDTH = 31
D_FF = 4 * D_MODEL
NORM_EPS = 1e-6

IN_SPLIT = (GLA_DK_TOTAL, GLA_DK_TOTAL, GLA_DV_TOTAL, GLA_DV_TOTAL, GLA_GATE_RANK,
            CONV_CHANNELS, CONV_CHANNELS, D_MODEL, D_MODEL)
D_IN = sum(IN_SPLIT)

kernel_name = "gla_conformer_conv_gated_hybrid_step"


def rmsnorm(x, g):
    xf = x.astype(jnp.float32)
    y = xf * lax.rsqrt(jnp.mean(xf * xf, axis=-1, keepdims=True) + NORM_EPS)
    return (y * g.astype(jnp.float32)).astype(x.dtype)


def layernorm(x, g, b):
    xf = x.astype(jnp.float32)
    mu = jnp.mean(xf, axis=-1, keepdims=True)
    xc = xf - mu
    y = xc * lax.rsqrt(jnp.mean(xc * xc, axis=-1, keepdims=True) + NORM_EPS)
    return (y * g.astype(jnp.float32) + b.astype(jnp.float32)).astype(x.dtype)


def gla_chunked(q, k, v, log_a, s0):
    B, T = q.shape[0], q.shape[1]
    n = -(-T // GLA_CHUNK)
    pad = n * GLA_CHUNK - T
    padw = ((0, 0), (0, pad), (0, 0), (0, 0))
    f32 = jnp.float32

    def to_chunks(t):
        t = jnp.pad(t.astype(f32), padw)
        return t.reshape(B, n, GLA_CHUNK, GLA_HEADS, t.shape[-1]).transpose(1, 0, 3, 2, 4)

    qc, kc, vc, ac = to_chunks(q), to_chunks(k), to_chunks(v), to_chunks(log_a)
    b = jnp.cumsum(ac, axis=-2)
    b_last = b[..., -1, :]
    q_dec = qc * jnp.exp(b)
    k_inv = kc * jnp.exp(-b)
    k_tail = kc * jnp.exp(b_last[..., None, :] - b)
    causal = jnp.tril(jnp.ones((GLA_CHUNK, GLA_CHUNK), dtype=bool))
    attn = jnp.where(causal, jnp.einsum('nbhid,nbhjd->nbhij', q_dec, k_inv), 0.0)
    o_intra = jnp.einsum('nbhij,nbhjv->nbhiv', attn, vc)

    def step(s, xs):
        qd, kt, vv, bl = xs
        o = jnp.einsum('bhid,bhdv->bhiv', qd, s)
        s = jnp.exp(bl)[..., None] * s + jnp.einsum('bhjd,bhjv->bhdv', kt, vv)
        return s, o

    s_final, o_inter = lax.scan(step, s0.astype(f32), (q_dec, k_tail, vc, b_last))
    o = (o_intra + o_inter).transpose(1, 0, 3, 2, 4).reshape(B, n * GLA_CHUNK, GLA_HEADS, GLA_DV)
    return o[:, :T], s_final


def trunk_layer(x, s_gla, s_conv, w_in, w_a2, b_a, g_gla_norm, w_o_gla, w_dw, b_dw,
                g_ln, b_ln, w_pw2, w_out, g_norm1, g_norm2, w_ff1, w_ff2):
    B, T, _ = x.shape
    h = rmsnorm(x, g_norm1)
    z = h @ w_in
    split_idx = np.cumsum(IN_SPLIT)[:-1].tolist()
    q, k, v, g, a_lr, glu_a, glu_b, gate_a, gate_b = jnp.split(z, split_idx, axis=-1)

    log_a = jax.nn.log_sigmoid((a_lr @ w_a2 + b_a).astype(jnp.float32)) / GLA_GATE_TAU
    q = q.reshape(B, T, GLA_HEADS, GLA_DK) * (GLA_DK ** -0.5)
    k = k.reshape(B, T, GLA_HEADS, GLA_DK)
    v = v.reshape(B, T, GLA_HEADS, GLA_DV)
    log_a = log_a.reshape(B, T, GLA_HEADS, GLA_DK)
    o, s_gla_new = gla_chunked(q, k, v, log_a, s_gla)
    o = rmsnorm(o, g_gla_norm).astype(x.dtype)
    o = o.reshape(B, T, GLA_DV_TOTAL) * jax.nn.silu(g)
    y_a = o @ w_o_gla

    u = glu_a * jax.nn.sigmoid(glu_b)
    u_ext = jnp.concatenate([s_conv.astype(u.dtype), u], axis=1)
    c = lax.conv_general_dilated(u_ext, w_dw[:, None, :].astype(u.dtype), window_strides=(1,),
                                 padding='VALID', dimension_numbers=('NWC', 'WIO', 'NWC'),
                                 feature_group_count=CONV_CHANNELS) + b_dw
    c = layernorm(c, g_ln, b_ln)
    y_b = jax.nn.silu(c) @ w_pw2
    s_conv_new = u_ext[:, -(CONV_WIDTH - 1):]

    mixed = jax.nn.sigmoid(gate_a) * y_a + jax.nn.sigmoid(gate_b) * y_b
    x = x + mixed @ w_out

    hf = rmsnorm(x, g_norm2)
    x = x + jnp.square(jax.nn.relu(hf @ w_ff1)) @ w_ff2
    return x, s_gla_new.astype(s_gla.dtype), s_conv_new.astype(s_conv.dtype)


def setup_inputs(seed: int = 0) -> dict:
    key = jax.random.key(seed)
    ks = jax.random.split(key, 24)
    nrm = jax.random.normal
    f32 = jnp.float32
    L = DEPTH
    return {
        "x_prompt": nrm(ks[0], (BATCH, SEQ, D_MODEL), f32),
        "x_sample": nrm(ks[1], (DEC_BATCH, DEC_SEQ, D_MODEL), f32),
        "state_gla": 2.0 * nrm(ks[2], (L, DEC_BATCH, GLA_HEADS, GLA_DK, GLA_DV), f32),
        "state_conv": 0.5 * nrm(ks[3], (L, DEC_BATCH, CONV_WIDTH - 1, CONV_CHANNELS), f32),
        "w_in": nrm(ks[4], (L, D_MODEL, D_IN), f32) * D_MODEL ** -0.5,
        "w_a2": nrm(ks[5], (L, GLA_GATE_RANK, GLA_DK_TOTAL), f32) * GLA_GATE_RANK ** -0.5,
        "b_a": 0.1 * nrm(ks[6], (L, GLA_DK_TOTAL), f32),
        "g_gla_norm": 1.0 + 0.1 * nrm(ks[7], (L, GLA_DV), f32),
        "w_o_gla": nrm(ks[8], (L, GLA_DV_TOTAL, D_MODEL), f32) * GLA_DV_TOTAL ** -0.5,
        "w_dw": nrm(ks[9], (L, CONV_WIDTH, CONV_CHANNELS), f32) * CONV_WIDTH ** -0.5,
        "b_dw": 0.02 * nrm(ks[10], (L, CONV_CHANNELS), f32),
        "g_ln": 1.0 + 0.1 * nrm(ks[11], (L, CONV_CHANNELS), f32),
        "b_ln": 0.02 * nrm(ks[12], (L, CONV_CHANNELS), f32),
        "w_pw2": nrm(ks[13], (L, CONV_CHANNELS, D_MODEL), f32) * CONV_CHANNELS ** -0.5,
        "w_out": nrm(ks[14], (L, D_MODEL, D_MODEL), f32) * D_MODEL ** -0.5,
        "g_norm1": 1.0 + 0.1 * nrm(ks[15], (L, D_MODEL), f32),
        "g_norm2": 1.0 + 0.1 * nrm(ks[16], (L, D_MODEL), f32),
        "w_ff1": nrm(ks[17], (L, D_MODEL, D_FF), f32) * D_MODEL ** -0.5,
        "w_ff2": nrm(ks[18], (L, D_FF, D_MODEL), f32) * D_FF ** -0.5,
        "g_final": 1.0 + 0.1 * nrm(ks[19], (D_MODEL,), f32),
    }


def reference(x_prompt, x_sample, state_gla, state_conv, w_in, w_a2, b_a, g_gla_norm, w_o_gla,
              w_dw, b_dw, g_ln, b_ln, w_pw2, w_out, g_norm1, g_norm2, w_ff1, w_ff2, g_final):
    hp, hs = x_prompt, x_sample
    gla_p, conv_p, gla_s, conv_s = [], [], [], []
    for l in range(DEPTH):
        w = (w_in[l], w_a2[l], b_a[l], g_gla_norm[l], w_o_gla[l], w_dw[l], b_dw[l],
             g_ln[l], b_ln[l], w_pw2[l], w_out[l], g_norm1[l], g_norm2[l], w_ff1[l], w_ff2[l])
        zg = jnp.zeros((hp.shape[0], GLA_HEADS, GLA_DK, GLA_DV), state_gla.dtype)
        zc = jnp.zeros((hp.shape[0], CONV_WIDTH - 1, CONV_CHANNELS), state_conv.dtype)
        hp, sgp, scp = trunk_layer(hp, zg, zc, *w)
        hs, sgs, scs = trunk_layer(hs, state_gla[l], state_conv[l], *w)
        gla_p.append(sgp); conv_p.append(scp); gla_s.append(sgs); conv_s.append(scs)
    y_prompt = rmsnorm(hp, g_final)
    y_sample = rmsnorm(hs, g_final)
    state_gla_prompt = jnp.stack(gla_p)
    state_conv_prompt = jnp.stack(conv_p)
    state_gla_sample = jnp.stack(gla_s)
    state_conv_sample = jnp.stack(conv_s)
    return (y_prompt, y_sample, state_gla_prompt, state_conv_prompt, state_gla_sample, state_conv_sample)
```

```python
import functools

import jax
import jax.numpy as jnp
from jax import lax
from jax.experimental import pallas as pl
from jax.experimental.pallas import tpu as pltpu

F32 = jnp.float32
BF16 = jnp.bfloat16

D_MODEL = 2048
GLA_HEADS = 4
GLA_DK = 256
GLA_DV = 512
GLA_DK_TOTAL = GLA_HEADS * GLA_DK
GLA_DV_TOTAL = GLA_HEADS * GLA_DV
GLA_GATE_RANK = 16
GLA_GATE_TAU = 16.0
CONV_WIDTH = 31
CONV_HALO = CONV_WIDTH - 1
D_FF = 4 * D_MODEL
NORM_EPS = 1e-6

LANES = 128
VMEM_LIMIT = 56 * 1024 * 1024

GLA_CHUNK = 128
GLA_SUB = 32
HALO_ROWS = 32
N_CBLK = D_MODEL // LANES


def _cparams(sem):
    return pltpu.CompilerParams(dimension_semantics=sem, vmem_limit_bytes=VMEM_LIMIT)


def _rms(x, g):
    ms = jnp.mean(x * x, axis=-1, keepdims=True)
    return x * lax.rsqrt(ms + NORM_EPS) * g


def _sigmoid(x):
    return 1.0 / (1.0 + jnp.exp(-x))


def _dot(a, b):
    return jnp.dot(a, b, preferred_element_type=F32)


def _dot_nt(a, b):
    return lax.dot_general(a, b, (((1,), (1,)), ((), ())), preferred_element_type=F32)


def _dot_tn(a, b):
    return lax.dot_general(a, b, (((0,), (0,)), ((), ())), preferred_element_type=F32)


def _norm_kernel(x_ref, g_ref, o_ref):
    o_ref[...] = _rms(x_ref[...], g_ref[...]).astype(o_ref.dtype)


def _norm_cast(x, g, tm):
    m, d = x.shape
    return pl.pallas_call(
        _norm_kernel,
        grid=(m // tm,),
        in_specs=[pl.BlockSpec((tm, d), lambda i: (i, 0)), pl.BlockSpec((1, d), lambda i: (0, 0))],
        out_specs=pl.BlockSpec((tm, d), lambda i: (i, 0)),
        out_shape=jax.ShapeDtypeStruct((m, d), BF16),
        compiler_params=_cparams(("parallel",)),
    )(x, g)


def _mm_kernel(*refs, n_lhs, pair_lhs, n_ext, n_const, epilogue):
    n_w = len(pair_lhs)
    lhs = refs[:n_lhs]
    ws = refs[n_lhs:n_lhs + n_w]
    ext = refs[n_lhs + n_w:n_lhs + n_w + n_ext]
    consts = refs[n_lhs + n_w + n_ext:n_lhs + n_w + n_ext + n_const]
    o_ref = refs[-1]
    accs = [_dot(lhs[li][...].astype(BF16), w[...]) for li, w in zip(pair_lhs, ws)]
    o_ref[...] = epilogue(accs, [e[...] for e in ext], [c[...] for c in consts]).astype(o_ref.dtype)


def _mm(lhs, ws, pair_lhs, ext, consts, epilogue, n_out, out_dtype, tm, tn, out_tn=None):
    m, k = lhs[0].shape
    out_tn = tn if out_tn is None else out_tn
    grid = (m // tm, n_out // out_tn)
    in_specs = [pl.BlockSpec((tm, k), lambda i, j: (i, 0)) for _ in lhs]
    in_specs += [pl.BlockSpec((k, tn), lambda i, j, off=off: (0, j + off)) for _, off in ws]
    in_specs += [pl.BlockSpec((tm, out_tn), lambda i, j, off=off: (i, j + off)) for _, off in ext]
    in_specs += [pl.BlockSpec(c.shape, lambda i, j, nd=c.ndim: (0,) * nd) for c in consts]
    kern = functools.partial(_mm_kernel, n_lhs=len(lhs), pair_lhs=tuple(pair_lhs), n_ext=len(ext),
                             n_const=len(consts), epilogue=epilogue)
    return pl.pallas_call(
        kern,
        grid=grid,
        in_specs=in_specs,
        out_specs=pl.BlockSpec((tm, out_tn), lambda i, j: (i, j)),
        out_shape=jax.ShapeDtypeStruct((m, n_out), out_dtype),
        compiler_params=_cparams(("parallel", "arbitrary")),
    )(*lhs, *[w for w, _ in ws], *[e for e, _ in ext], *consts)


def _ep_identity(accs, ext, consts):
    return accs[0]


def _ep_silu(accs, ext, consts):
    g = accs[0]
    return g * _sigmoid(g)


def _ep_sigmoid(accs, ext, consts):
    return _sigmoid(accs[0])


def _ep_glu(accs, ext, consts):
    return accs[0] * _sigmoid(accs[1])


def _ep_log_decay(accs, ext, consts):
    w_a2, b_a = consts
    x = _dot(accs[0].astype(BF16), w_a2) + b_a
    log_sig = jnp.minimum(x, 0.0) - jnp.log1p(jnp.exp(-jnp.abs(x)))
    return log_sig / GLA_GATE_TAU


def _ep_mix(accs, ext, consts):
    ga, gb = ext
    return ga.astype(F32) * accs[0] + gb.astype(F32) * accs[1]


def _ep_residual(accs, ext, consts):
    return ext[0] + accs[0]


def _gla_prompt_kernel(q_ref, k_ref, v_ref, la_ref, sg_ref, gn_ref, og_ref, s_out_ref, st_ref, a_ref):
    t = pl.program_id(1)
    c_len = la_ref.shape[1]

    @pl.when(t == 0)
    def _():
        st_ref[...] = jnp.zeros_like(st_ref)
        a_ref[...] = jnp.zeros_like(a_ref)

    la = la_ref[0]
    hi = la.astype(BF16)
    r1 = la - hi.astype(F32)
    mid = r1.astype(BF16)
    lo = (r1 - mid.astype(F32)).astype(BF16)
    row = lax.broadcasted_iota(jnp.int32, (c_len, c_len), 0)
    col = lax.broadcasted_iota(jnp.int32, (c_len, c_len), 1)
    tri = (row >= col).astype(BF16)
    b_all = _dot(tri, hi) + _dot(tri, mid) + _dot(tri, lo)

    n_sub = c_len // GLA_SUB
    for h in range(GLA_HEADS):
        ks = slice(h * GLA_DK, (h + 1) * GLA_DK)
        vs = slice(h * GLA_DV, (h + 1) * GLA_DV)
        b = b_all[:, ks]
        bl = b[c_len - 1:c_len, :]
        q = q_ref[0, :, ks].astype(F32) * (GLA_DK ** -0.5)
        k = k_ref[0, :, ks].astype(F32)
        v = v_ref[0, :, vs]
        st = st_ref[h]

        qd = (q * jnp.exp(b)).astype(BF16)
        o = _dot_nt(qd, st.astype(BF16))

        for blk in range(n_sub):
            r0 = blk * GLA_SUB
            ncol = r0 + GLA_SUB
            mrow = r0 + GLA_SUB // 2 - 1
            m = b[mrow:mrow + 1, :]
            q_i = (q[r0:ncol] * jnp.exp(b[r0:ncol] - m)).astype(BF16)
            k_i = (k[:ncol] * jnp.exp(m - b[:ncol])).astype(BF16)
            a_i = _dot_nt(q_i, k_i)
            rr = lax.broadcasted_iota(jnp.int32, (GLA_SUB, ncol), 0) + r0
            cc = lax.broadcasted_iota(jnp.int32, (GLA_SUB, ncol), 1)
            a_ref[h, r0:ncol, 0:ncol] = jnp.where(rr >= cc, a_i, 0.0)
        o = o + _dot(a_ref[h].astype(BF16), v)

        kt = (k * jnp.exp(bl - b)).astype(BF16)
        st_ref[h] = st * jnp.exp(bl) + _dot_tn(v, kt)

        on = _rms(o, gn_ref[...])
        og_ref[0, :, vs] = (on * sg_ref[0, :, vs].astype(F32)).astype(og_ref.dtype)

    @pl.when(t == pl.num_programs(1) - 1)
    def _():
        for h in range(GLA_HEADS):
            s_out_ref[0, h] = st_ref[h].T


def _gla_prompt(qkv, la, sg, gn):
    bsz, t_len, _ = la.shape
    c = GLA_CHUNK
    grid = (bsz, t_len // c)
    return pl.pallas_call(
        _gla_prompt_kernel,
        grid=grid,
        in_specs=[
            pl.BlockSpec((1, c, GLA_DK_TOTAL), lambda b, t: (b, t, 0)),
            pl.BlockSpec((1, c, GLA_DK_TOTAL), lambda b, t: (b, t, 1)),
            pl.BlockSpec((1, c, GLA_DV_TOTAL), lambda b, t: (b, t, 1)),
            pl.BlockSpec((1, c, GLA_DK_TOTAL), lambda b, t: (b, t, 0)),
            pl.BlockSpec((1, c, GLA_DV_TOTAL), lambda b, t: (b, t, 0)),
            pl.BlockSpec((1, GLA_DV), lambda b, t: (0, 0)),
        ],
        out_specs=[
            pl.BlockSpec((1, c, GLA_DV_TOTAL), lambda b, t: (b, t, 0)),
            pl.BlockSpec((1, GLA_HEADS, GLA_DK, GLA_DV), lambda b, t: (b, 0, 0, 0)),
        ],
        out_shape=[
            jax.ShapeDtypeStruct((bsz, t_len, GLA_DV_TOTAL), BF16),
            jax.ShapeDtypeStruct((bsz, GLA_HEADS, GLA_DK, GLA_DV), F32),
        ],
        scratch_shapes=[
            pltpu.VMEM((GLA_HEADS, GLA_DV, GLA_DK), F32),
            pltpu.VMEM((GLA_HEADS, c, c), F32),
        ],
        compiler_params=_cparams(("parallel", "arbitrary")),
    )(qkv, qkv, qkv, la, sg, gn)


GLA_S_ROWS = 16


def _pad_rows(x, rows):
    return jnp.concatenate([x, jnp.zeros((rows - x.shape[0], x.shape[1]), x.dtype)], axis=0)


def _gla_sample_kernel(q_ref, k_ref, v_ref, la_ref, sg_ref, gn_ref, s_ref, og_ref, s_out_ref, *, t_len):
    bb = s_ref.shape[0]
    rows = GLA_S_ROWS
    rr = lax.broadcasted_iota(jnp.int32, (rows, rows), 0)
    cc = lax.broadcasted_iota(jnp.int32, (rows, rows), 1)
    causal = rr >= cc
    ones = jnp.ones((rows, LANES), BF16)
    for bi in range(bb):
        ts = slice(bi * t_len, (bi + 1) * t_len)
        for h in range(GLA_HEADS):
            ks = slice(h * GLA_DK, (h + 1) * GLA_DK)
            vs = slice(h * GLA_DV, (h + 1) * GLA_DV)
            la = la_ref[ts, ks]
            cum = [la[0:1]]
            for i in range(1, t_len):
                cum.append(cum[-1] + la[i:i + 1])
            bl = cum[-1]
            b = jnp.concatenate(cum + [jnp.broadcast_to(bl, (rows - t_len, GLA_DK))], axis=0)
            q = _pad_rows(q_ref[ts, ks], rows) * (GLA_DK ** -0.5)
            k = _pad_rows(k_ref[ts, ks], rows)
            v = _pad_rows(v_ref[ts, vs], rows).astype(BF16)
            s = s_ref[bi, h]

            qd = (q * jnp.exp(b)).astype(BF16)
            k_inv = (k * jnp.exp(-b)).astype(BF16)
            attn = jnp.where(causal, _dot_nt(qd, k_inv), 0.0)
            o = _dot(attn.astype(BF16), v) + _dot(qd, s.astype(BF16))

            hi = la.astype(BF16)
            r1 = la - hi.astype(F32)
            mid = r1.astype(BF16)
            lo = (r1 - mid.astype(F32)).astype(BF16)
            la3 = _pad_rows(jnp.concatenate([hi.astype(F32), mid.astype(F32), lo.astype(F32)], axis=0),
                            rows).astype(BF16)
            bl_col = _dot_tn(la3, ones)
            decay = jnp.exp(bl_col)
            decay = jnp.concatenate([decay] * (GLA_DV // LANES), axis=1)

            kt = (k * jnp.exp(bl - b)).astype(BF16)
            s_out_ref[bi, h] = decay * s + _dot_tn(kt, v)

            on = _rms(o[0:t_len], gn_ref[...])
            og_ref[ts, vs] = on * sg_ref[ts, vs]


def _gla_sample(qkv, la, sg, gn, s0, t_len, bb):
    m = la.shape[0]
    bsz = m // t_len
    r = bb * t_len
    return pl.pallas_call(
        functools.partial(_gla_sample_kernel, t_len=t_len),
        grid=(bsz // bb,),
        in_specs=[
            pl.BlockSpec((r, GLA_DK_TOTAL), lambda i: (i, 0)),
            pl.BlockSpec((r, GLA_DK_TOTAL), lambda i: (i, 1)),
            pl.BlockSpec((r, GLA_DV_TOTAL), lambda i: (i, 1)),
            pl.BlockSpec((r, GLA_DK_TOTAL), lambda i: (i, 0)),
            pl.BlockSpec((r, GLA_DV_TOTAL), lambda i: (i, 0)),
            pl.BlockSpec((1, GLA_DV), lambda i: (0, 0)),
            pl.BlockSpec((bb, GLA_HEADS, GLA_DK, GLA_DV), lambda i: (i, 0, 0, 0)),
        ],
        out_specs=[
            pl.BlockSpec((r, GLA_DV_TOTAL), lambda i: (i, 0)),
            pl.BlockSpec((bb, GLA_HEADS, GLA_DK, GLA_DV), lambda i: (i, 0, 0, 0)),
        ],
        out_shape=[
            jax.ShapeDtypeStruct((m, GLA_DV_TOTAL), F32),
            jax.ShapeDtypeStruct((bsz, GLA_HEADS, GLA_DK, GLA_DV), F32),
        ],
        compiler_params=_cparams(("parallel",)),
    )(qkv, qkv, qkv, la, sg, gn, s0)


def _ln_silu(c, g, b):
    mu = jnp.mean(c, axis=-1, keepdims=True)
    xc = c - mu
    var = jnp.mean(xc * xc, axis=-1, keepdims=True)
    y = xc * lax.rsqrt(var + NORM_EPS) * g + b
    return y * _sigmoid(y)


def _conv_prompt_kernel(u_ref, halo_ref, w_ref, bdw_ref, gln_ref, bln_ref, o_ref, ext_ref, c_ref):
    t = pl.program_id(1)
    tm = u_ref.shape[1]
    first = t == 0
    for cb in range(N_CBLK):
        cs = slice(cb * LANES, (cb + 1) * LANES)
        ext_ref[cb, 0:HALO_ROWS, :] = jnp.where(first, 0.0, halo_ref[0, :, cs])
        ext_ref[cb, HALO_ROWS:HALO_ROWS + tm, :] = u_ref[0, :, cs]

    off = HALO_ROWS - CONV_HALO

    def body(cb, carry):
        acc = jnp.zeros((tm, LANES), F32)
        for j in range(CONV_WIDTH):
            acc = acc + w_ref[cb, j:j + 1, :] * ext_ref[cb, off + j:off + j + tm, :]
        c_ref[cb] = acc
        return carry

    lax.fori_loop(0, N_CBLK, body, 0)

    c = jnp.concatenate([c_ref[cb] for cb in range(N_CBLK)], axis=1) + bdw_ref[...]
    o_ref[0] = _ln_silu(c, gln_ref[...], bln_ref[...]).astype(o_ref.dtype)


def _conv_prompt(u, w_blk, b_dw, g_ln, b_ln, tm):
    bsz, t_len, ch = u.shape
    hb = tm // HALO_ROWS
    return pl.pallas_call(
        _conv_prompt_kernel,
        grid=(bsz, t_len // tm),
        in_specs=[
            pl.BlockSpec((1, tm, ch), lambda b, t: (b, t, 0)),
            pl.BlockSpec((1, HALO_ROWS, ch), lambda b, t: (b, jnp.maximum(t * hb - 1, 0), 0)),
            pl.BlockSpec((N_CBLK, CONV_WIDTH, LANES), lambda b, t: (0, 0, 0)),
            pl.BlockSpec((1, ch), lambda b, t: (0, 0)),
            pl.BlockSpec((1, ch), lambda b, t: (0, 0)),
            pl.BlockSpec((1, ch), lambda b, t: (0, 0)),
        ],
        out_specs=pl.BlockSpec((1, tm, ch), lambda b, t: (b, t, 0)),
        out_shape=jax.ShapeDtypeStruct((bsz, t_len, ch), BF16),
        scratch_shapes=[
            pltpu.VMEM((N_CBLK, HALO_ROWS + tm, LANES), F32),
            pltpu.VMEM((N_CBLK, tm, LANES), F32),
        ],
        compiler_params=_cparams(("parallel", "arbitrary")),
    )(u, u, w_blk, b_dw, g_ln, b_ln)


def _conv_sample_kernel(sc_ref, u_ref, w_ref, bdw_ref, gln_ref, bln_ref, o_ref, sc_out_ref, ext_ref):
    bb, t_len, _ = u_ref.shape

    def body(bi, carry):
        ext_ref[0:CONV_HALO, :] = sc_ref[bi]
        ext_ref[CONV_HALO:CONV_HALO + t_len, :] = u_ref[bi]
        acc = jnp.zeros((t_len, D_MODEL), F32)
        for j in range(CONV_WIDTH):
            acc = acc + w_ref[j:j + 1, :] * ext_ref[j:j + t_len, :]
        c = acc + bdw_ref[...]
        o_ref[bi] = _ln_silu(c, gln_ref[...], bln_ref[...]).astype(o_ref.dtype)
        sc_out_ref[bi] = ext_ref[t_len:t_len + CONV_HALO, :]
        return carry

    lax.fori_loop(0, bb, body, 0)


def _conv_sample(sc, u, w_dw, b_dw, g_ln, b_ln, bb):
    bsz, t_len, ch = u.shape
    ext_rows = -(-(CONV_HALO + t_len) // 8) * 8
    return pl.pallas_call(
        _conv_sample_kernel,
        grid=(bsz // bb,),
        in_specs=[
            pl.BlockSpec((bb, CONV_HALO, ch), lambda i: (i, 0, 0)),
            pl.BlockSpec((bb, t_len, ch), lambda i: (i, 0, 0)),
            pl.BlockSpec((CONV_WIDTH, ch), lambda i: (0, 0)),
            pl.BlockSpec((1, ch), lambda i: (0, 0)),
            pl.BlockSpec((1, ch), lambda i: (0, 0)),
            pl.BlockSpec((1, ch), lambda i: (0, 0)),
        ],
        out_specs=[
            pl.BlockSpec((bb, t_len, ch), lambda i: (i, 0, 0)),
            pl.BlockSpec((bb, CONV_HALO, ch), lambda i: (i, 0, 0)),
        ],
        out_shape=[
            jax.ShapeDtypeStruct((bsz, t_len, ch), F32),
            jax.ShapeDtypeStruct((bsz, CONV_HALO, ch), F32),
        ],
        scratch_shapes=[pltpu.VMEM((ext_rows, ch), F32)],
        compiler_params=_cparams(("parallel",)),
    )(sc, u, w_dw, b_dw, g_ln, b_ln)


def _ffn_kernel(x_ref, g2_ref, gf_ref, w1_ref, w2_ref, o_ref, hf_ref):
    j = pl.program_id(1)

    @pl.when(j == 0)
    def _():
        x = x_ref[...]
        hf_ref[...] = _rms(x, g2_ref[...]).astype(hf_ref.dtype)
        o_ref[...] = x

    h = jnp.square(jnp.maximum(_dot(hf_ref[...], w1_ref[...]), 0.0)).astype(BF16)
    o_ref[...] += _dot(h, w2_ref[...])

    @pl.when(j == pl.num_programs(1) - 1)
    def _():
        o_ref[...] = _rms(o_ref[...], gf_ref[...])


def _ffn(x, g2, gf, w1, w2, tm, tf):
    m, d = x.shape
    dff = w1.shape[1]
    return pl.pallas_call(
        _ffn_kernel,
        grid=(m // tm, dff // tf),
        in_specs=[
            pl.BlockSpec((tm, d), lambda i, j: (i, 0)),
            pl.BlockSpec((1, d), lambda i, j: (0, 0)),
            pl.BlockSpec((1, d), lambda i, j: (0, 0)),
            pl.BlockSpec((d, tf), lambda i, j: (0, j)),
            pl.BlockSpec((tf, d), lambda i, j: (j, 0)),
        ],
        out_specs=pl.BlockSpec((tm, d), lambda i, j: (i, 0)),
        out_shape=jax.ShapeDtypeStruct((m, d), F32),
        scratch_shapes=[pltpu.VMEM((tm, d), BF16)],
        compiler_params=_cparams(("parallel", "arbitrary")),
    )(x, g2, gf, w1, w2)


def _layer(x, s_gla, s_conv, w, tm):
    bsz, t_len, d = x.shape
    m = bsz * t_len
    tn = 512
    x2 = x.reshape(m, d)
    fresh = s_gla is None
    act = BF16 if fresh else F32

    h = _norm_cast(x2, w["g_norm1"], min(tm, 512))
    qkv = _mm([h], [(w["w_qkv"], 0)], [0], [], [], _ep_identity, 2 * GLA_DK_TOTAL + GLA_DV_TOTAL, act, tm, tn)
    sg = _mm([h], [(w["w_g"], 0)], [0], [], [], _ep_silu, GLA_DV_TOTAL, act, tm, tn)
    u = _mm([h], [(w["w_glu_a"], 0), (w["w_glu_b"], 0)], [0, 0], [], [], _ep_glu, D_MODEL, F32, tm, tn)
    gates = _mm([h], [(w["w_gates"], 0)], [0], [], [], _ep_sigmoid, 2 * D_MODEL, BF16, tm, tn)
    la = _mm([h], [(w["w_alr"], 0)], [0], [], [w["w_a2"], w["b_a"]], _ep_log_decay, GLA_DK_TOTAL, F32,
             tm, LANES, out_tn=GLA_DK_TOTAL)

    u3 = u.reshape(bsz, t_len, -1)
    if fresh:
        og, s_gla_new = _gla_prompt(qkv.reshape(bsz, t_len, -1), la.reshape(bsz, t_len, -1),
                                    sg.reshape(bsz, t_len, -1), w["g_gla_norm"])
        cs = _conv_prompt(u3, w["w_dw_blk"], w["b_dw"], w["g_ln"], w["b_ln"], 256)
        s_conv_new = u3[:, t_len - CONV_HALO:, :]
    else:
        og, s_gla_new = _gla_sample(qkv, la, sg, w["g_gla_norm"], s_gla, t_len, 2)
        cs, s_conv_new = _conv_sample(s_conv, u3, w["w_dw"], w["b_dw"], w["g_ln"], w["b_ln"], 8)

    nj = D_MODEL // tn
    mixed = _mm([og.reshape(m, -1), cs.reshape(m, -1)], [(w["w_o_gla"], 0), (w["w_pw2"], 0)], [0, 1],
                [(gates, 0), (gates, nj)], [], _ep_mix, D_MODEL, BF16, tm, tn)
    x1 = _mm([mixed], [(w["w_out"], 0)], [0], [(x2, 0)], [], _ep_residual, D_MODEL, F32, tm, tn)
    y = _ffn(x1, w["g_norm2"], w["g_final"], w["w_ff1"], w["w_ff2"], tm, 512)
    return y.reshape(bsz, t_len, d), s_gla_new, s_conv_new


def _prep_weights(w_in, w_a2, b_a, g_gla_norm, w_o_gla, w_dw, b_dw, g_ln, b_ln, w_pw2, w_out,
                  g_norm1, g_norm2, w_ff1, w_ff2, g_final):
    wi = w_in[0]
    c_q, c_k, c_v, c_g = GLA_DK_TOTAL, GLA_DK_TOTAL, GLA_DV_TOTAL, GLA_DV_TOTAL
    o_alr = c_q + c_k + c_v + c_g
    o_glu_a = o_alr + GLA_GATE_RANK
    o_glu_b = o_glu_a + D_MODEL
    o_gates = o_glu_b + D_MODEL
    row = lambda a: a.reshape(1, -1)
    return {
        "w_qkv": wi[:, :c_q + c_k + c_v].astype(BF16),
        "w_g": wi[:, c_q + c_k + c_v:o_alr].astype(BF16),
        "w_alr": jnp.pad(wi[:, o_alr:o_glu_a], ((0, 0), (0, LANES - GLA_GATE_RANK))).astype(BF16),
        "w_glu_a": wi[:, o_glu_a:o_glu_b].astype(BF16),
        "w_glu_b": wi[:, o_glu_b:o_gates].astype(BF16),
        "w_gates": wi[:, o_gates:].astype(BF16),
        "w_a2": jnp.pad(w_a2[0], ((0, LANES - GLA_GATE_RANK), (0, 0))).astype(BF16),
        "b_a": row(b_a[0]),
        "g_gla_norm": row(g_gla_norm[0]),
        "w_o_gla": w_o_gla[0].astype(BF16),
        "w_dw": w_dw[0],
        "w_dw_blk": w_dw[0].reshape(CONV_WIDTH, N_CBLK, LANES).transpose(1, 0, 2),
        "b_dw": row(b_dw[0]),
        "g_ln": row(g_ln[0]),
        "b_ln": row(b_ln[0]),
        "w_pw2": w_pw2[0].astype(BF16),
        "w_out": w_out[0].astype(BF16),
        "g_norm1": row(g_norm1[0]),
        "g_norm2": row(g_norm2[0]),
        "w_ff1": w_ff1[0].astype(BF16),
        "w_ff2": w_ff2[0].astype(BF16),
        "g_final": row(g_final),
    }


def kernel(x_prompt, x_sample, state_gla, state_conv, w_in, w_a2, b_a, g_gla_norm, w_o_gla, w_dw, b_dw,
           g_ln, b_ln, w_pw2, w_out, g_norm1, g_norm2, w_ff1, w_ff2, g_final):
    assert w_in.shape[0] == 1, "single-layer trunk"
    w = _prep_weights(w_in, w_a2, b_a, g_gla_norm, w_o_gla, w_dw, b_dw, g_ln, b_ln, w_pw2, w_out,
                      g_norm1, g_norm2, w_ff1, w_ff2, g_final)
    y_p, sg_p, sc_p = _layer(x_prompt, None, None, w, 1024)
    y_s, sg_s, sc_s = _layer(x_sample, state_gla[0], state_conv[0], w, 512)
    return (y_p, y_s, sg_p[None], sc_p[None], sg_s[None], sc_s[None])
```
